```python
import math
import jax, jax.numpy as jnp
from jax import lax
import numpy as np

D_MODEL = 1024
BATCH = 4
SEQ = 8192
DEPTH = 1
DEC_BATCH = 128
DEC_SEQ = 1
PAST_LEN = 8192
PAGE_SIZE = 128

CONV_WIDTH = D_MODEL
CONV_K = 3
N_HEADS = 8
HEAD_DIM = D_MODEL // (2 * N_HEADS)
N_QK = 2 * N_HEADS
QK_WIDTH = N_QK * HEAD_DIM
V_DIM = 2 * HEAD_DIM
ATTN_WIDTH = N_HEADS * V_DIM
Q_BLOCK = 128
SUBLN_EPS = 1e-5
N_BUCKETS = 32
MAX_EXACT = N_BUCKETS // 2
MAX_DISTANCE = 128
PEER_HEADS = 8
N_KEYS = 128
N_EXPERTS = N_KEYS * N_KEYS
PEER_QDIM = 256
HALF_QDIM = PEER_QDIM // 2
PEER_TOPK = 16
PEER_BLOCK = 128
N_IN = 3 * CONV_WIDTH + 2 * QK_WIDTH + ATTN_WIDTH + 2 * D_MODEL
EPS = 1e-6
NEG_INF = -1e30

kernel_name = 'hybrid_shortconv_diffattn_peer_step'


def rmsnorm(x, g, eps=EPS):
    xf = x.astype(jnp.float32)
    xf = xf * lax.rsqrt(jnp.mean(xf * xf, axis=-1, keepdims=True) + eps)
    return (xf * g.astype(jnp.float32)).astype(x.dtype)


def lambda_init(layer):
    return 0.8 - 0.6 * math.exp(-0.3 * layer)


def rel_bucket(qpos, kpos):
    n = jnp.maximum(qpos[:, None] - kpos[None, :], 0)
    nf = jnp.maximum(n, 1).astype(jnp.float32)
    large = MAX_EXACT + (jnp.log(nf / MAX_EXACT) / math.log(MAX_DISTANCE / MAX_EXACT)
                         * (N_BUCKETS - MAX_EXACT)).astype(jnp.int32)
    large = jnp.minimum(large, N_BUCKETS - 1)
    return jnp.where(n < MAX_EXACT, n, large)


def diff_attn_core(q, k, v, qpos, kpos, rel_bias, lam):
    B, Q, K = q.shape[0], q.shape[1], k.shape[1]
    s = jnp.einsum('bqgd,bkgd->bgqk', q.astype(jnp.float32), k.astype(jnp.float32)) / math.sqrt(HEAD_DIM)
    s = s.reshape(B, N_HEADS, 2, Q, K)
    bias = rel_bias.astype(jnp.float32)[rel_bucket(qpos, kpos)]
    bias = jnp.transpose(bias, (2, 0, 1))[None, :, None]
    mask = kpos[None, :] <= qpos[:, None]
    p = jax.nn.softmax(jnp.where(mask, s + bias, NEG_INF), axis=-1)
    a = p[:, :, 0] - lam * p[:, :, 1]
    return jnp.einsum('bhqk,bkhe->bqhe', a, v.astype(jnp.float32))


def prompt_attention(q, k, v, rel_bias, lam):
    B, S = q.shape[0], q.shape[1]
    nb = S // Q_BLOCK
    pos = jnp.arange(S, dtype=jnp.int32)
    qb = jnp.moveaxis(q.reshape(B, nb, Q_BLOCK, N_QK, HEAD_DIM), 1, 0)
    pb = pos.reshape(nb, Q_BLOCK)

    def blk(args):
        qi, pi = args
        return diff_attn_core(qi, k, v, pi, pos, rel_bias, lam)

    o = lax.map(blk, (qb, pb))
    return jnp.moveaxis(o, 0, 1).reshape(B, S, N_HEADS, V_DIM)


def sample_attention(q, k_new, v_new, cache_k, cache_v, layer, page_table, rel_bias, lam):
    T = q.shape[1]
    past = page_table.shape[1] * PAGE_SIZE
    qpos = past + jnp.arange(T, dtype=jnp.int32)
    kpos = jnp.arange(past + T, dtype=jnp.int32)

    def one(args):
        qi, ki, vi, pages = args
        kp = cache_k[layer, pages].reshape(past, N_QK, HEAD_DIM)
        vp = cache_v[layer, pages].reshape(past, N_HEADS, V_DIM)
        kk = jnp.concatenate([kp, ki.astype(kp.dtype)], axis=0)[None]
        vv = jnp.concatenate([vp, vi.astype(vp.dtype)], axis=0)[None]
        return diff_attn_core(qi[None], kk, vv, qpos, kpos, rel_bias, lam)[0]

    return lax.map(one, (q, k_new, v_new, page_table))


def short_conv(u, buf, w):
    L = u.shape[1]
    up = jnp.concatenate([buf.astype(u.dtype), u], axis=1)
    out = sum(w[j] * up[:, j:j + L] for j in range(CONV_K))
    return out, up[:, L:]


def mixer(h, conv_buf, attn_fn, w_in, b_gate, conv_w, w_conv_out, subln, w_attn_out, w_o, lam_init):
    B, L, _ = h.shape
    z = h @ w_in
    c = CONV_WIDTH
    splits = [c, 2 * c, 3 * c, 3 * c + QK_WIDTH, 3 * c + 2 * QK_WIDTH, 3 * c + 2 * QK_WIDTH + ATTN_WIDTH]
    zb, zc, zx, zq, zk, zv, zg = jnp.split(z, splits, axis=-1)
    cout, new_buf = short_conv(zc * zx, conv_buf, conv_w)
    a = (zb * cout) @ w_conv_out
    q = zq.reshape(B, L, N_QK, HEAD_DIM)
    k = zk.reshape(B, L, N_QK, HEAD_DIM)
    v = zv.reshape(B, L, N_HEADS, V_DIM)
    o = attn_fn(q, k, v)
    o = (rmsnorm(o, subln, SUBLN_EPS) * (1.0 - lam_init)).astype(h.dtype).reshape(B, L, ATTN_WIDTH)
    b = o @ w_attn_out
    ga, gb = jnp.split(jax.nn.sigmoid(zg + b_gate), 2, axis=-1)
    return (ga * a + gb * b) @ w_o, new_buf, k, v


def peer_tokens(xn, w_q, keys1, keys2, u_tab, v_tab):
    T = xn.shape[0]
    q = (xn @ w_q).reshape(T, PEER_HEADS, 2, HALF_QDIM).astype(jnp.float32)
    s1 = jnp.einsum('thd,hnd->thn', q[:, :, 0], keys1.astype(jnp.float32))
    s2 = jnp.einsum('thd,hnd->thn', q[:, :, 1], keys2.astype(jnp.float32))
    v1, i1 = lax.top_k(s1, PEER_TOPK)
    v2, i2 = lax.top_k(s2, PEER_TOPK)
    cand = (v1[..., :, None] + v2[..., None, :]).reshape(T, PEER_HEADS, PEER_TOPK * PEER_TOPK)
    cidx = (i1[..., :, None] * N_KEYS + i2[..., None, :]).reshape(T, PEER_HEADS, PEER_TOPK * PEER_TOPK)
    best, pos = lax.top_k(cand, PEER_TOPK)
    idx = jnp.take_along_axis(cidx, pos, axis=-1)
    g = jax.nn.softmax(best, axis=-1)
    act = jax.nn.gelu(jnp.einsum('thkd,td->thk', u_tab[idx], xn).astype(jnp.float32), approximate=False)
    return jnp.einsum('thk,thkd->td', (g * act).astype(xn.dtype), v_tab[idx])


def peer_blocked(xn, w_q, keys1, keys2, u_tab, v_tab):
    T = xn.shape[0]
    xb = xn.reshape(T // PEER_BLOCK, PEER_BLOCK, D_MODEL)
    out = lax.map(lambda xi: peer_tokens(xi, w_q, keys1, keys2, u_tab, v_tab), xb)
    return out.reshape(T, D_MODEL)


def setup_inputs(seed: int = 0) -> dict:
    key = jax.random.key(seed)
    ks = jax.random.split(key, 26)
    f32 = jnp.float32
    n_pages = PAST_LEN // PAGE_SIZE
    n_used = DEC_BATCH * n_pages
    n_phys = n_used + n_used // 4

    def nrm(k, shape, scale):
        return jax.random.normal(k, shape, f32) * scale

    page_table = jax.random.permutation(ks[5], n_phys)[:n_used].reshape(DEC_BATCH, n_pages).astype(jnp.int32)
    return {
        'x_prompt': nrm(ks[0], (BATCH, SEQ, D_MODEL), 1.0),
        'x_sample': nrm(ks[1], (DEC_BATCH, DEC_SEQ, D_MODEL), 1.0),
        'cache_k': nrm(ks[2], (DEPTH, n_phys, PAGE_SIZE, N_QK, HEAD_DIM), 1.0),
        'cache_v': nrm(ks[3], (DEPTH, n_phys, PAGE_SIZE, N_HEADS, V_DIM), 1.0),
        'state_conv': nrm(ks[4], (DEPTH, DEC_BATCH, CONV_K - 1, CONV_WIDTH), 1.0),
        'page_table': page_table,
        'norm_mix': 1.0 + nrm(ks[6], (DEPTH, D_MODEL), 0.05),
        'w_in': nrm(ks[7], (DEPTH, D_MODEL, N_IN), D_MODEL ** -0.5),
        'b_gate': nrm(ks[8], (DEPTH, 2 * D_MODEL), 0.1),
        'conv_w': nrm(ks[9], (DEPTH, CONV_K, CONV_WIDTH), 0.5),
        'w_conv_out': nrm(ks[10], (DEPTH, CONV_WIDTH, D_MODEL), CONV_WIDTH ** -0.5),
        'lambda_q1': nrm(ks[11], (DEPTH, HEAD_DIM), 0.1),
        'lambda_k1': nrm(ks[12], (DEPTH, HEAD_DIM), 0.1),
        'lambda_q2': nrm(ks[13], (DEPTH, HEAD_DIM), 0.1),
        'lambda_k2': nrm(ks[14], (DEPTH, HEAD_DIM), 0.1),
        'subln': 1.0 + nrm(ks[15], (DEPTH, V_DIM), 0.05),
        'w_attn_out': nrm(ks[16], (DEPTH, ATTN_WIDTH, D_MODEL), ATTN_WIDTH ** -0.5),
        'w_o': nrm(ks[17], (DEPTH, D_MODEL, D_MODEL), D_MODEL ** -0.5),
        'rel_bias': nrm(ks[18], (N_BUCKETS, N_HEADS), 0.5),
        'norm_ffn': 1.0 + nrm(ks[19], (DEPTH, D_MODEL), 0.05),
        'peer_w_q': nrm(ks[20], (DEPTH, D_MODEL, PEER_HEADS * PEER_QDIM), D_MODEL ** -0.5),
        'peer_keys1': nrm(ks[21], (DEPTH, PEER_HEADS, N_KEYS, HALF_QDIM), HALF_QDIM ** -0.5),
        'peer_keys2': nrm(ks[22], (DEPTH, PEER_HEADS, N_KEYS, HALF_QDIM), HALF_QDIM ** -0.5),
        'peer_u': nrm(ks[23], (DEPTH, N_EXPERTS, D_MODEL), D_MODEL ** -0.5),
        'peer_v': nrm(ks[24], (DEPTH, N_EXPERTS, D_MODEL), (PEER_HEADS * PEER_TOPK) ** -0.5),
        'norm_final': 1.0 + nrm(ks[25], (D_MODEL,), 0.05),
    }


def reference(x_prompt, x_sample, cache_k, cache_v, state_conv, page_table,
              norm_mix, w_in, b_gate, conv_w, w_conv_out,
              lambda_q1, lambda_k1, lambda_q2, lambda_k2, subln, w_attn_out, w_o, rel_bias,
              norm_ffn, peer_w_q, peer_keys1, peer_keys2, peer_u, peer_v, norm_final):
    f32 = jnp.float32
    xp, xs = x_prompt, x_sample
    kp_l, vp_l, cp_l, ks_l, vs_l, cs_l = [], [], [], [], [], []
    for l in range(DEPTH):
        lam_init = lambda_init(l)
        lam = (jnp.exp(jnp.sum(lambda_q1[l].astype(f32) * lambda_k1[l].astype(f32)))
               - jnp.exp(jnp.sum(lambda_q2[l].astype(f32) * lambda_k2[l].astype(f32))) + lam_init)
        attn_p = lambda q, k, v: prompt_attention(q, k, v, rel_bias, lam)
        attn_s = lambda q, k, v: sample_attention(q, k, v, cache_k, cache_v, l, page_table, rel_bias, lam)
        mix_w = (w_in[l], b_gate[l], conv_w[l], w_conv_out[l], subln[l], w_attn_out[l], w_o[l], lam_init)
        peer_w = (peer_w_q[l], peer_keys1[l], peer_keys2[l], peer_u[l], peer_v[l])
        buf0 = jnp.zeros((xp.shape[0], CONV_K - 1, CONV_WIDTH), xp.dtype)
        m, cbuf, k, v = mixer(rmsnorm(xp, norm_mix[l]), buf0, attn_p, *mix_w)
        xp = xp + m
        xp = xp + peer_blocked(rmsnorm(xp, norm_ffn[l]).reshape(-1, D_MODEL), *peer_w).reshape(xp.shape)
        kp_l.append(k); vp_l.append(v); cp_l.append(cbuf)
        m, cbuf, k, v = mixer(rmsnorm(xs, norm_mix[l]), state_conv[l], attn_s, *mix_w)
        xs = xs + m
        xs = xs + peer_tokens(rmsnorm(xs, norm_ffn[l]).reshape(-1, D_MODEL), *peer_w).reshape(xs.shape)
        ks_l.append(k); vs_l.append(v); cs_l.append(cbuf)
    y_prompt = rmsnorm(xp, norm_final)
    y_sample = rmsnorm(xs, norm_final)
    return (y_prompt, y_sample, jnp.stack(kp_l), jnp.stack(vp_l), jnp.stack(cp_l),
            jnp.stack(ks_l), jnp.stack(vs_l), jnp.stack(cs_l))
```

```python
import functools
import math

import jax
import jax.numpy as jnp
from jax import lax
from jax.experimental import pallas as pl
from jax.experimental.pallas import tpu as pltpu

F32 = jnp.float32
BF16 = jnp.bfloat16
I32 = jnp.int32

D_MODEL = 1024
CONV_K = 3
N_HEADS = 8
HEAD_DIM = 64
N_QK = 2 * N_HEADS
V_DIM = 2 * HEAD_DIM
PAGE_SIZE = 128
SUBLN_EPS = 1e-5
N_BUCKETS = 32
MAX_EXACT = N_BUCKETS // 2
MAX_DISTANCE = 128
PEER_HEADS = 8
N_KEYS = 128
PEER_QDIM = 256
HALF_QDIM = PEER_QDIM // 2
PEER_TOPK = 16
PEER_PAIRS = PEER_HEADS * PEER_TOPK
N_IN = 8 * D_MODEL
COL_B, COL_C, COL_X, COL_Q, COL_K, COL_V, COL_GA, COL_GB = range(8)
EPS = 1e-6
NEG_INF = -1e30

LANES = 128
PACK_ROWS = D_MODEL // (2 * LANES)
ATTN_BLOCK = 512
VMEM_LIMIT = 56 * 1024 * 1024

_NT = (((1,), (1,)), ((), ()))


def _params(semantics, vmem=VMEM_LIMIT):
    return pltpu.CompilerParams(dimension_semantics=semantics, vmem_limit_bytes=vmem)


def _rms(x, g, eps):
    return x * lax.rsqrt(jnp.mean(x * x, axis=-1, keepdims=True) + eps) * g


def _in_proj_kernel(x_ref, g_ref, w_ref, z_ref, qkv_ref, h_ref):
    j = pl.program_id(1)

    @pl.when(j == 0)
    def _():
        h_ref[...] = _rms(x_ref[...], g_ref[...], EPS).astype(BF16)

    z = jnp.dot(h_ref[...], w_ref[...], preferred_element_type=F32)
    z_ref[...] = z

    @pl.when((j >= COL_Q) & (j <= COL_V))
    def _():
        qkv_ref[...] = z.astype(BF16)


def _in_proj(x2, g, w_bf):
    t = x2.shape[0]
    tm = min(1024, t)
    return pl.pallas_call(
        _in_proj_kernel,
        grid=(t // tm, N_IN // D_MODEL),
        in_specs=[
            pl.BlockSpec((tm, D_MODEL), lambda i, j: (i, 0)),
            pl.BlockSpec((1, D_MODEL), lambda i, j: (0, 0)),
            pl.BlockSpec((D_MODEL, D_MODEL), lambda i, j: (0, j)),
        ],
        out_specs=[
            pl.BlockSpec((tm, D_MODEL), lambda i, j: (i, j)),
            pl.BlockSpec((tm, D_MODEL), lambda i, j: (i, jnp.clip(j - COL_Q, 0, 2))),
        ],
        out_shape=[
            jax.ShapeDtypeStruct((t, N_IN), F32),
            jax.ShapeDtypeStruct((t, 3 * D_MODEL), BF16),
        ],
        scratch_shapes=[pltpu.VMEM((tm, D_MODEL), BF16)],
        compiler_params=_params(("arbitrary", "arbitrary")),
        name="in_proj",
    )(x2, g, w_bf)


def _conv_seq_kernel(zb_ref, zc_ref, zx_ref, zg_ref, bg_ref, cw_ref, wco_ref, buf_ref,
                     ag_ref, nbuf_ref, carry_ref):
    tm = zb_ref.shape[0]

    @pl.when(pl.program_id(1) == 0)
    def _():
        carry_ref[...] = buf_ref[0]

    u = zc_ref[...] * zx_ref[...]
    prev = carry_ref[...]
    row = lax.broadcasted_iota(I32, (tm, 1), 0)
    u1 = jnp.where(row == 0, prev[1:2], pltpu.roll(u, 1, 0))
    u2 = jnp.where(row == 0, prev[0:1], jnp.where(row == 1, prev[1:2], pltpu.roll(u, 2, 0)))
    cw = cw_ref[...]
    cout = cw[0:1] * u2 + cw[1:2] * u1 + cw[2:3] * u
    a = jnp.dot((zb_ref[...] * cout).astype(BF16), wco_ref[...], preferred_element_type=F32)
    ag_ref[...] = jax.nn.sigmoid(zg_ref[...] + bg_ref[...]) * a
    last = u[tm - 2:tm]
    carry_ref[...] = last
    nbuf_ref[0] = last


def _conv_seq(z, bsz, seqlen, buf, bg_a, cw, wco_bf):
    tm = min(512, seqlen)
    nl = seqlen // tm

    def col(c):
        return pl.BlockSpec((tm, D_MODEL), lambda b, l: (b * nl + l, c))

    return pl.pallas_call(
        _conv_seq_kernel,
        grid=(bsz, nl),
        in_specs=[
            col(COL_B), col(COL_C), col(COL_X), col(COL_GA),
            pl.BlockSpec((1, D_MODEL), lambda b, l: (0, 0)),
            pl.BlockSpec((CONV_K, D_MODEL), lambda b, l: (0, 0)),
            pl.BlockSpec((D_MODEL, D_MODEL), lambda b, l: (0, 0)),
            pl.BlockSpec((1, CONV_K - 1, D_MODEL), lambda b, l: (b, 0, 0)),
        ],
        out_specs=[
            pl.BlockSpec((tm, D_MODEL), lambda b, l: (b * nl + l, 0)),
            pl.BlockSpec((1, CONV_K - 1, D_MODEL), lambda b, l: (b, 0, 0)),
        ],
        out_shape=[
            jax.ShapeDtypeStruct((bsz * seqlen, D_MODEL), F32),
            jax.ShapeDtypeStruct((bsz, CONV_K - 1, D_MODEL), F32),
        ],
        scratch_shapes=[pltpu.VMEM((CONV_K - 1, D_MODEL), F32)],
        compiler_params=_params(("arbitrary", "arbitrary")),
        name="conv_seq",
    )(z, z, z, z, bg_a, cw, wco_bf, buf)


def _conv_step_kernel(zb_ref, zc_ref, zx_ref, zg_ref, bg_ref, cw_ref, wco_ref, buf_ref,
                      ag_ref, nbuf_ref):
    u = zc_ref[...] * zx_ref[...]
    b0 = buf_ref[:, :D_MODEL]
    b1 = buf_ref[:, D_MODEL:]
    cw = cw_ref[...]
    cout = cw[0:1] * b0 + cw[1:2] * b1 + cw[2:3] * u
    a = jnp.dot((zb_ref[...] * cout).astype(BF16), wco_ref[...], preferred_element_type=F32)
    ag_ref[...] = jax.nn.sigmoid(zg_ref[...] + bg_ref[...]) * a
    nbuf_ref[:, :D_MODEL] = b1
    nbuf_ref[:, D_MODEL:] = u


def _conv_step(z, buf, bg_a, cw, wco_bf):
    t = z.shape[0]

    def col(c):
        return pl.BlockSpec((t, D_MODEL), lambda i: (0, c))

    ag, nbuf = pl.pallas_call(
        _conv_step_kernel,
        grid=(1,),
        in_specs=[
            col(COL_B), col(COL_C), col(COL_X), col(COL_GA),
            pl.BlockSpec((1, D_MODEL), lambda i: (0, 0)),
            pl.BlockSpec((CONV_K, D_MODEL), lambda i: (0, 0)),
            pl.BlockSpec((D_MODEL, D_MODEL), lambda i: (0, 0)),
            pl.BlockSpec((t, 2 * D_MODEL), lambda i: (0, 0)),
        ],
        out_specs=[
            pl.BlockSpec((t, D_MODEL), lambda i: (0, 0)),
            pl.BlockSpec((t, 2 * D_MODEL), lambda i: (0, 0)),
        ],
        out_shape=[
            jax.ShapeDtypeStruct((t, D_MODEL), F32),
            jax.ShapeDtypeStruct((t, 2 * D_MODEL), F32),
        ],
        compiler_params=_params(("arbitrary",)),
        name="conv_step",
    )(z, z, z, z, bg_a, cw, wco_bf, buf.reshape(t, 2 * D_MODEL))
    return ag, nbuf.reshape(t, CONV_K - 1, D_MODEL)


def _bucket_bias(dist, rb_ref, h):
    n = jnp.maximum(dist, 0)
    nf = jnp.maximum(n, 1).astype(F32)
    large = MAX_EXACT + (jnp.log(nf / MAX_EXACT) / math.log(MAX_DISTANCE / MAX_EXACT)
                         * (N_BUCKETS - MAX_EXACT)).astype(I32)
    large = jnp.minimum(large, N_BUCKETS - 1)
    bucket = jnp.where(n < MAX_EXACT, n, large)
    out = jnp.zeros(dist.shape, F32)
    for b in range(N_BUCKETS):
        out = jnp.where(bucket == b, rb_ref[b, h], out)
    return out


def _bias_tiles_kernel(rb_ref, out_ref):
    d = pl.program_id(0)
    h = pl.program_id(1)
    tb = out_ref.shape[2]
    q = lax.broadcasted_iota(I32, (tb, tb), 0)
    k = lax.broadcasted_iota(I32, (tb, tb), 1)
    dist = q - k + d * tb
    out_ref[0, 0] = jnp.where(dist < 0, NEG_INF, _bucket_bias(dist, rb_ref, h))


def _bias_tiles(rel_bias, tb):
    assert tb >= MAX_DISTANCE
    return pl.pallas_call(
        _bias_tiles_kernel,
        grid=(3, N_HEADS),
        in_specs=[pl.BlockSpec(memory_space=pltpu.SMEM)],
        out_specs=pl.BlockSpec((1, 1, tb, tb), lambda d, h: (d, h, 0, 0)),
        out_shape=jax.ShapeDtypeStruct((3, N_HEADS, tb, tb), F32),
        compiler_params=_params(("arbitrary", "arbitrary")),
        name="bias_tiles",
    )(rel_bias)


def _bias_rows_kernel(rb_ref, out_ref, *, past):
    n = out_ref.shape[1]
    dist = past - lax.broadcasted_iota(I32, (1, n), 1)
    for g in range(N_QK):
        out_ref[g:g + 1, :] = _bucket_bias(dist, rb_ref, g // 2)


def _bias_rows(rel_bias, past):
    return pl.pallas_call(
        functools.partial(_bias_rows_kernel, past=past),
        in_specs=[pl.BlockSpec(memory_space=pltpu.SMEM)],
        out_specs=pl.BlockSpec(memory_space=pltpu.VMEM),
        out_shape=jax.ShapeDtypeStruct((N_QK, past), F32),
        name="bias_rows",
    )(rel_bias)


def _finish_head(acc_pos, l_pos, acc_neg, l_neg, lam, subln, lam_init):
    o = acc_pos / l_pos - lam * (acc_neg / l_neg)
    return (_rms(o, subln, SUBLN_EPS) * (1.0 - lam_init)).astype(BF16)


def _attn_kernel(qi_ref, ki_ref, q_ref, k_ref, v_ref, bias_ref, lam_ref, subln_ref, o_ref,
                 qm_ref, m_ref, l_ref, acc_ref, *, lam_init):
    s = pl.program_id(1)
    qi = qi_ref[s]
    ki = ki_ref[s]
    tb = q_ref.shape[0]

    @pl.when(ki == 0)
    def _():
        q = q_ref[...].astype(F32) * (1.0 / math.sqrt(HEAD_DIM))
        lane = lax.broadcasted_iota(I32, q.shape, 1)
        first = (lane & (V_DIM - 1)) < HEAD_DIM
        qm_ref[0] = jnp.where(first, q, 0.0).astype(BF16)
        qm_ref[1] = jnp.where(first, 0.0, q).astype(BF16)
        m_ref[...] = jnp.full(m_ref.shape, -jnp.inf, F32)
        l_ref[...] = jnp.zeros(l_ref.shape, F32)
        acc_ref[...] = jnp.zeros(acc_ref.shape, F32)

    for h in range(N_HEADS):
        cols = slice(h * V_DIM, (h + 1) * V_DIM)
        kb = k_ref[:, cols]
        vb = v_ref[:, cols]
        bias = bias_ref[0, h]
        for j in range(2):
            g = 2 * h + j
            sc = lax.dot_general(qm_ref[j, :, cols], kb, _NT, preferred_element_type=F32) + bias
            m_prev = m_ref[g]
            m_new = jnp.maximum(m_prev, jnp.max(sc, axis=1, keepdims=True))
            alpha = jnp.exp(m_prev - m_new)
            p = jnp.exp(sc - m_new)
            l_ref[g] = alpha * l_ref[g] + jnp.sum(p, axis=1, keepdims=True)
            acc_ref[g] = alpha * acc_ref[g] + jnp.dot(p.astype(BF16), vb, preferred_element_type=F32)
            m_ref[g] = m_new

    @pl.when(ki == qi)
    def _():
        lam = lam_ref[0, 0]
        for h in range(N_HEADS):
            o_ref[:, h * V_DIM:(h + 1) * V_DIM] = _finish_head(
                acc_ref[2 * h], l_ref[2 * h], acc_ref[2 * h + 1], l_ref[2 * h + 1],
                lam, subln_ref[...], lam_init)


def _prompt_attention(qkv, bsz, seqlen, bias_tiles, lam, subln, lam_init):
    tb = bias_tiles.shape[2]
    nb = seqlen // tb
    pairs = [(qi, ki) for qi in range(nb) for ki in range(qi + 1)]
    qi_tab = jnp.asarray([p[0] for p in pairs], I32)
    ki_tab = jnp.asarray([p[1] for p in pairs], I32)

    grid_spec = pltpu.PrefetchScalarGridSpec(
        num_scalar_prefetch=2,
        grid=(bsz, len(pairs)),
        in_specs=[
            pl.BlockSpec((tb, D_MODEL), lambda b, s, qi, ki: (b * nb + qi[s], 0)),
            pl.BlockSpec((tb, D_MODEL), lambda b, s, qi, ki: (b * nb + ki[s], 1)),
            pl.BlockSpec((tb, D_MODEL), lambda b, s, qi, ki: (b * nb + ki[s], 2)),
            pl.BlockSpec((1, N_HEADS, tb, tb),
                         lambda b, s, qi, ki: (jnp.minimum(qi[s] - ki[s], 2), 0, 0, 0)),
            pl.BlockSpec(memory_space=pltpu.SMEM),
            pl.BlockSpec((1, V_DIM), lambda b, s, qi, ki: (0, 0)),
        ],
        out_specs=pl.BlockSpec((tb, D_MODEL), lambda b, s, qi, ki: (b * nb + qi[s], 0)),
        scratch_shapes=[
            pltpu.VMEM((2, tb, D_MODEL), BF16),
            pltpu.VMEM((N_QK, tb, 1), F32),
            pltpu.VMEM((N_QK, tb, 1), F32),
            pltpu.VMEM((N_QK, tb, V_DIM), F32),
        ],
    )
    return pl.pallas_call(
        functools.partial(_attn_kernel, lam_init=lam_init),
        grid_spec=grid_spec,
        out_shape=jax.ShapeDtypeStruct((bsz * seqlen, D_MODEL), BF16),
        compiler_params=_params(("arbitrary", "arbitrary")),
        name="prompt_attention",
    )(qi_tab, ki_tab, qkv, qkv, qkv, bias_tiles, lam, subln)


PAGES_PER_STEP = 4


def _decode_kernel(pt_ref, *refs, lam_init):
    npg = PAGES_PER_STEP
    k_refs = refs[:npg]
    v_refs = refs[npg:2 * npg]
    (z_ref, bias_ref, b0_ref, lam_ref, subln_ref, o_ref, qbd_ref, m_ref, l_ref, acc_ref) = refs[2 * npg:]
    step = pl.program_id(1)

    @pl.when(step == 0)
    def _():
        q = z_ref[0, :, COL_Q * D_MODEL:(COL_Q + 1) * D_MODEL] * (1.0 / math.sqrt(HEAD_DIM))
        lane = lax.broadcasted_iota(I32, (N_QK, D_MODEL), 1)
        row = lax.broadcasted_iota(I32, (N_QK, D_MODEL), 0)
        qbd_ref[...] = jnp.where((lane >> 6) == row, q, 0.0).astype(BF16)
        m_ref[...] = jnp.full(m_ref.shape, -jnp.inf, F32)
        l_ref[...] = jnp.zeros(l_ref.shape, F32)
        acc_ref[...] = jnp.zeros(acc_ref.shape, F32)

    qbd = qbd_ref[...]
    sc = jnp.concatenate(
        [lax.dot_general(qbd, k_refs[i][0].astype(BF16), _NT, preferred_element_type=F32)
         for i in range(npg)], axis=1) + bias_ref[...]
    m_prev = m_ref[...]
    m_new = jnp.maximum(m_prev, jnp.max(sc, axis=1, keepdims=True))
    alpha = jnp.exp(m_prev - m_new)
    p = jnp.exp(sc - m_new)
    l_ref[...] = alpha * l_ref[...] + jnp.sum(p, axis=1, keepdims=True)
    pv = jnp.dot(p[:, :PAGE_SIZE].astype(BF16), v_refs[0][0].astype(BF16), preferred_element_type=F32)
    for i in range(1, npg):
        pv += jnp.dot(p[:, i * PAGE_SIZE:(i + 1) * PAGE_SIZE].astype(BF16), v_refs[i][0].astype(BF16),
                      preferred_element_type=F32)
    acc_ref[...] = alpha * acc_ref[...] + pv
    m_ref[...] = m_new

    @pl.when(step == pl.num_programs(1) - 1)
    def _():
        k_new = z_ref[0, :, COL_K * D_MODEL:(COL_K + 1) * D_MODEL].astype(BF16).astype(F32)
        v_new = z_ref[0, :, COL_V * D_MODEL:(COL_V + 1) * D_MODEL].astype(BF16).astype(F32)
        s_new = jnp.sum(qbd_ref[...].astype(F32) * k_new, axis=1, keepdims=True) + b0_ref[...]
        m_prev = m_ref[...]
        m_fin = jnp.maximum(m_prev, s_new)
        alpha = jnp.exp(m_prev - m_fin)
        p_new = jnp.exp(s_new - m_fin)
        l_fin = alpha * l_ref[...] + p_new
        acc = alpha * acc_ref[...] + p_new.astype(BF16).astype(F32) * v_new
        lam = lam_ref[0, 0]
        for h in range(N_HEADS):
            cols = slice(h * V_DIM, (h + 1) * V_DIM)
            o_ref[0, :, cols] = _finish_head(
                acc[2 * h:2 * h + 1, cols], l_fin[2 * h:2 * h + 1],
                acc[2 * h + 1:2 * h + 2, cols], l_fin[2 * h + 1:2 * h + 2],
                lam, subln_ref[...], lam_init)


def _sample_attention(z, cache_k, cache_v, page_table, bias_rows, bias_now, lam, subln, lam_init):
    nseq, n_pages = page_table.shape
    npg = PAGES_PER_STEP
    n_phys = cache_k.shape[0]
    ck = cache_k.reshape(n_phys, PAGE_SIZE, D_MODEL)
    cv = cache_v.reshape(n_phys, PAGE_SIZE, D_MODEL)

    def page(i):
        return pl.BlockSpec((1, PAGE_SIZE, D_MODEL), lambda b, s, pt: (pt[b * n_pages + s * npg + i], 0, 0))

    grid_spec = pltpu.PrefetchScalarGridSpec(
        num_scalar_prefetch=1,
        grid=(nseq, n_pages // npg),
        in_specs=(
            [page(i) for i in range(npg)] + [page(i) for i in range(npg)] + [
                pl.BlockSpec((1, 1, N_IN), lambda b, s, pt: (b, 0, 0)),
                pl.BlockSpec((N_QK, npg * PAGE_SIZE), lambda b, s, pt: (0, s)),
                pl.BlockSpec((N_QK, 1), lambda b, s, pt: (0, 0)),
                pl.BlockSpec(memory_space=pltpu.SMEM),
                pl.BlockSpec((1, V_DIM), lambda b, s, pt: (0, 0)),
            ]),
        out_specs=pl.BlockSpec((1, 1, D_MODEL), lambda b, s, pt: (b, 0, 0)),
        scratch_shapes=[
            pltpu.VMEM((N_QK, D_MODEL), BF16),
            pltpu.VMEM((N_QK, 1), F32),
            pltpu.VMEM((N_QK, 1), F32),
            pltpu.VMEM((N_QK, D_MODEL), F32),
        ],
    )
    o = pl.pallas_call(
        functools.partial(_decode_kernel, lam_init=lam_init),
        grid_spec=grid_spec,
        out_shape=jax.ShapeDtypeStruct((nseq, 1, D_MODEL), BF16),
        compiler_params=_params(("arbitrary", "arbitrary")),
        name="sample_attention",
    )(page_table.reshape(-1), *([ck] * npg), *([cv] * npg), z.reshape(nseq, 1, N_IN),
      bias_rows, bias_now, lam, subln)
    return o.reshape(nseq, D_MODEL)


def _post_kernel(o_ref, ag_ref, zg_ref, x_ref, bg_ref, wao_ref, wo_ref, gf_ref, wq_ref,
                 x1_ref, xn_ref, qp_ref):
    b = jnp.dot(o_ref[...], wao_ref[...], preferred_element_type=F32)
    mix = ag_ref[...] + jax.nn.sigmoid(zg_ref[...] + bg_ref[...]) * b
    x1 = x_ref[...] + jnp.dot(mix.astype(BF16), wo_ref[...], preferred_element_type=F32)
    x1_ref[...] = x1
    xn = _rms(x1, gf_ref[...], EPS)
    xn_ref[...] = xn
    qp_ref[...] = jnp.dot(xn.astype(BF16), wq_ref[...], preferred_element_type=F32).astype(BF16)


def _post(o_n, ag, z, x2, bg_b, wao_bf, wo_bf, g_ffn, wq_bf):
    t = x2.shape[0]
    tm = min(256, t)
    nq = PEER_HEADS * PEER_QDIM

    def rows(width, col=0):
        return pl.BlockSpec((tm, width), lambda i: (i, col))

    def whole(shape):
        return pl.BlockSpec(shape, lambda i: (0, 0))

    return pl.pallas_call(
        _post_kernel,
        grid=(t // tm,),
        in_specs=[
            rows(D_MODEL), rows(D_MODEL), rows(D_MODEL, COL_GB), rows(D_MODEL),
            whole((1, D_MODEL)), whole((D_MODEL, D_MODEL)), whole((D_MODEL, D_MODEL)),
            whole((1, D_MODEL)), whole((D_MODEL, nq)),
        ],
        out_specs=[rows(D_MODEL), rows(D_MODEL), rows(nq)],
        out_shape=[
            jax.ShapeDtypeStruct((t, D_MODEL), F32),
            jax.ShapeDtypeStruct((t, D_MODEL), F32),
            jax.ShapeDtypeStruct((t, nq), BF16),
        ],
        compiler_params=_params(("arbitrary",)),
        name="post",
    )(o_n, ag, z, x2, bg_b, wao_bf, wo_bf, g_ffn, wq_bf)


def _top_k_rows(s, k, vals_ref, ids_ref, ids):
    for i in range(k):
        m = jnp.max(s, axis=0, keepdims=True)
        am = jnp.min(jnp.where(s == m, ids, jnp.inf), axis=0, keepdims=True)
        vals_ref[i:i + 1, :] = m
        ids_ref[i:i + 1, :] = am
        s = jnp.where(ids == am, -jnp.inf, s)


def _route_kernel(qp_ref, k1_ref, k2_ref, idx_ref, gate_ref, v1_ref, i1_ref, v2_ref, i2_ref,
                  best_ref, pos_ref, idxt_ref, gatet_ref):
    h = pl.program_id(1)
    tb = qp_ref.shape[0]
    key_ids = lax.broadcasted_iota(I32, (N_KEYS, tb), 0).astype(F32)
    for half, (k_ref, v_ref, i_ref) in enumerate(((k1_ref, v1_ref, i1_ref), (k2_ref, v2_ref, i2_ref))):
        q = qp_ref[:, half * HALF_QDIM:(half + 1) * HALF_QDIM]
        s = lax.dot_general(k_ref[0], q, _NT, preferred_element_type=F32)
        _top_k_rows(s, PEER_TOPK, v_ref, i_ref, key_ids)
    v1, i1, v2, i2 = v1_ref[...], i1_ref[...], v2_ref[...], i2_ref[...]

    kk = PEER_TOPK
    b_ids = lax.broadcasted_iota(I32, (kk, tb), 0)
    cands, flat = [], []
    for a in range(kk // 2):
        cands.append(jnp.where(b_ids < kk // (a + 1), v1[a:a + 1] + v2, -jnp.inf))
        flat.append((b_ids + a * kk).astype(F32))
    cands.append(v1[kk // 2:] + v2[0:1])
    flat.append(((lax.broadcasted_iota(I32, (kk // 2, tb), 0) + kk // 2) * kk).astype(F32))
    _top_k_rows(jnp.concatenate(cands, axis=0), kk, best_ref, pos_ref, jnp.concatenate(flat, axis=0))

    best = best_ref[...]
    pos = pos_ref[...].astype(I32)
    a_sel, b_sel = pos >> 4, pos & (kk - 1)
    rank = lax.broadcasted_iota(I32, (kk, tb), 0)
    for i in range(kk):
        e1 = jnp.sum(jnp.where(rank == a_sel[i:i + 1], i1, 0.0), axis=0, keepdims=True)
        e2 = jnp.sum(jnp.where(rank == b_sel[i:i + 1], i2, 0.0), axis=0, keepdims=True)
        idxt_ref[pl.ds(h * kk + i, 1), :] = (e1 * N_KEYS + e2) * PACK_ROWS
    e = jnp.exp(best - best[0:1])
    gatet_ref[pl.ds(h * kk, kk), :] = e / jnp.sum(e, axis=0, keepdims=True)

    @pl.when(h == PEER_HEADS - 1)
    def _():
        idx_ref[...] = idxt_ref[...].T.astype(I32)
        gate_ref[...] = gatet_ref[...].T


def _route(qp, k1_bf, k2_bf):
    t = qp.shape[0]
    tb = min(256, t)
    kk = PEER_TOPK
    return pl.pallas_call(
        _route_kernel,
        grid=(t // tb, PEER_HEADS),
        in_specs=[
            pl.BlockSpec((tb, PEER_QDIM), lambda i, h: (i, h)),
            pl.BlockSpec((1, N_KEYS, HALF_QDIM), lambda i, h: (h, 0, 0)),
            pl.BlockSpec((1, N_KEYS, HALF_QDIM), lambda i, h: (h, 0, 0)),
        ],
        out_specs=[
            pl.BlockSpec((tb, PEER_PAIRS), lambda i, h: (i, 0)),
            pl.BlockSpec((tb, PEER_PAIRS), lambda i, h: (i, 0)),
        ],
        out_shape=[
            jax.ShapeDtypeStruct((t, PEER_PAIRS), I32),
            jax.ShapeDtypeStruct((t, PEER_PAIRS), F32),
        ],
        scratch_shapes=[pltpu.VMEM((kk, tb), F32)] * 6 + [pltpu.VMEM((PEER_PAIRS, tb), F32)] * 2,
        compiler_params=_params(("arbitrary", "arbitrary")),
        name="route",
    )(qp, k1_bf, k2_bf)


def _pack_table(tab):
    n = tab.shape[0]
    bits = lax.bitcast_convert_type(tab.astype(BF16), jnp.uint16).astype(jnp.uint32)
    half = D_MODEL // 2
    words = bits[:, :half] | (bits[:, half:] << 16)
    return lax.bitcast_convert_type(words, I32).reshape(n * PACK_ROWS, LANES)


def _unpack_row(tab_ref, row0):
    words = tab_ref[pl.ds(pl.multiple_of(row0, PACK_ROWS), PACK_ROWS), :]
    lo = lax.bitcast_convert_type(words << 16, F32)
    hi = lax.bitcast_convert_type(words & jnp.int32(-65536), F32)
    return lo, hi


def _gelu(x):
    return 0.5 * x * (1.0 + lax.erf(x * math.sqrt(0.5)))


def _peer_up_kernel(idx_ref, tab_ref, xn_ref, gate_ref, w_ref, part_ref, act_ref):
    tb = xn_ref.shape[0]
    ones = jnp.ones((8, LANES), BF16)

    def token(t, carry):
        x = xn_ref[t]
        x_lo, x_hi = x[:PACK_ROWS], x[PACK_ROWS:]
        for p in range(PEER_PAIRS):
            lo, hi = _unpack_row(tab_ref, idx_ref[t, p])
            part_ref[p:p + 1, :] = jnp.sum(lo * x_lo + hi * x_hi, axis=0, keepdims=True)
        part = part_ref[...]
        p_hi = part.astype(BF16)
        p_lo = (part - p_hi.astype(F32)).astype(BF16)
        sums = (lax.dot_general(ones, p_hi, _NT, preferred_element_type=F32)
                + lax.dot_general(ones, p_lo, _NT, preferred_element_type=F32))
        act_ref[pl.ds(t, 1), :] = sums[0:1]
        return carry

    lax.fori_loop(0, tb, token, 0)
    w_ref[...] = gate_ref[...] * _gelu(act_ref[...])


PEER_TOKENS_PER_STEP = 32


def _peer_up(idx, tab, xn, gate):
    t = xn.shape[0]
    tb = min(PEER_TOKENS_PER_STEP, t)
    return pl.pallas_call(
        _peer_up_kernel,
        grid=(t // tb,),
        in_specs=[
            pl.BlockSpec((tb, PEER_PAIRS), lambda i: (i, 0), memory_space=pltpu.SMEM),
            pl.BlockSpec(memory_space=pltpu.VMEM),
            pl.BlockSpec((tb, 8, LANES), lambda i: (i, 0, 0)),
            pl.BlockSpec((tb, PEER_PAIRS), lambda i: (i, 0)),
        ],
        out_specs=pl.BlockSpec((tb, PEER_PAIRS), lambda i: (i, 0)),
        out_shape=jax.ShapeDtypeStruct((t, PEER_PAIRS), F32),
        scratch_shapes=[pltpu.VMEM((PEER_PAIRS, LANES), F32), pltpu.VMEM((tb, PEER_PAIRS), F32)],
        compiler_params=_params(("arbitrary",)),
        name="peer_up",
    )(idx, tab, xn.reshape(t, 8, LANES), gate)


N_PARTIAL = 4


def _peer_down_kernel(idx_ref, w_ref, tab_ref, x1_ref, gf_ref, y_ref):
    tb = x1_ref.shape[0]

    def token(t, carry):
        acc_lo = [jnp.zeros((PACK_ROWS, LANES), F32) for _ in range(N_PARTIAL)]
        acc_hi = [jnp.zeros((PACK_ROWS, LANES), F32) for _ in range(N_PARTIAL)]
        for p in range(PEER_PAIRS):
            lo, hi = _unpack_row(tab_ref, idx_ref[t, p])
            w = w_ref[t, p]
            acc_lo[p % N_PARTIAL] += w * lo
            acc_hi[p % N_PARTIAL] += w * hi
        out = jnp.concatenate([sum(acc_lo[1:], acc_lo[0]), sum(acc_hi[1:], acc_hi[0])], axis=0)
        x = x1_ref[t] + out
        ms = jnp.sum(jnp.sum(x * x, axis=1, keepdims=True), axis=0, keepdims=True) / D_MODEL
        y_ref[t] = x * lax.rsqrt(ms + EPS) * gf_ref[...]
        return carry

    lax.fori_loop(0, tb, token, 0)


def _peer_down(idx, w, tab, x1, g_final):
    t = x1.shape[0]
    tb = min(PEER_TOKENS_PER_STEP, t)
    y = pl.pallas_call(
        _peer_down_kernel,
        grid=(t // tb,),
        in_specs=[
            pl.BlockSpec((tb, PEER_PAIRS), lambda i: (i, 0), memory_space=pltpu.SMEM),
            pl.BlockSpec((tb, PEER_PAIRS), lambda i: (i, 0), memory_space=pltpu.SMEM),
            pl.BlockSpec(memory_space=pltpu.VMEM),
            pl.BlockSpec((tb, 8, LANES), lambda i: (i, 0, 0)),
            pl.BlockSpec((8, LANES), lambda i: (0, 0)),
        ],
        out_specs=pl.BlockSpec((tb, 8, LANES), lambda i: (i, 0, 0)),
        out_shape=jax.ShapeDtypeStruct((t, 8, LANES), F32),
        compiler_params=_params(("arbitrary",)),
        name="peer_down",
    )(idx, w, tab, x1.reshape(t, 8, LANES), g_final.reshape(8, LANES))
    return y.reshape(t, D_MODEL)


def _lambda_init(layer):
    return 0.8 - 0.6 * math.exp(-0.3 * layer)


def _layer_group(x2, attn_fn, conv_fn, w):
    z, qkv = _in_proj(x2, w["norm_mix"], w["w_in"])
    ag, nbuf = conv_fn(z)
    o_n = attn_fn(z, qkv)
    x1, xn, qp = _post(o_n, ag, z, x2, w["bg_b"], w["w_attn_out"], w["w_o"], w["norm_ffn"], w["peer_w_q"])
    idx, gate = _route(qp, w["keys1"], w["keys2"])
    pw = _peer_up(idx, w["u_tab"], xn, gate)
    return z, nbuf, x1, idx, pw


def kernel(x_prompt, x_sample, cache_k, cache_v, state_conv, page_table, norm_mix, w_in, b_gate, conv_w, w_conv_out, lambda_q1, lambda_k1, lambda_q2, lambda_k2, subln, w_attn_out, w_o, rel_bias, norm_ffn, peer_w_q, peer_keys1, peer_keys2, peer_u, peer_v, norm_final):
    depth = w_in.shape[0]
    assert depth == 1, "the final norm is fused into the last (only) layer's PEER kernel"
    bsz, seqlen, _ = x_prompt.shape
    nseq, dec_len, _ = x_sample.shape
    assert dec_len == 1
    past = page_table.shape[1] * PAGE_SIZE
    tb = min(ATTN_BLOCK, seqlen)

    bias_tiles = _bias_tiles(rel_bias, tb)
    bias_rows = _bias_rows(rel_bias, past)
    bias_now = jnp.repeat(rel_bias[0], 2).reshape(N_QK, 1)

    xp = x_prompt.reshape(bsz * seqlen, D_MODEL)
    xs = x_sample.reshape(nseq, D_MODEL)
    outs = []
    for l in range(depth):
        lam_init = _lambda_init(l)
        lam = (jnp.exp(jnp.sum(lambda_q1[l] * lambda_k1[l])) - jnp.exp(jnp.sum(lambda_q2[l] * lambda_k2[l]))
               + lam_init).reshape(1, 1)
        sub = subln[l].reshape(1, V_DIM)
        w = dict(
            norm_mix=norm_mix[l].reshape(1, D_MODEL), w_in=w_in[l].astype(BF16),
            bg_b=b_gate[l, D_MODEL:].reshape(1, D_MODEL),
            w_attn_out=w_attn_out[l].astype(BF16), w_o=w_o[l].astype(BF16),
            norm_ffn=norm_ffn[l].reshape(1, D_MODEL), peer_w_q=peer_w_q[l].astype(BF16),
            keys1=peer_keys1[l].astype(BF16), keys2=peer_keys2[l].astype(BF16),
            u_tab=_pack_table(peer_u[l]),
        )
        v_tab = _pack_table(peer_v[l])
        bg_a = b_gate[l, :D_MODEL].reshape(1, D_MODEL)
        wco = w_conv_out[l].astype(BF16)
        buf0 = jnp.zeros((bsz, CONV_K - 1, D_MODEL), F32)

        zp, cbuf_p, x1p, idx_p, pw_p = _layer_group(
            xp,
            lambda z, qkv: _prompt_attention(qkv, bsz, seqlen, bias_tiles, lam, sub, lam_init),
            lambda z: _conv_seq(z, bsz, seqlen, buf0, bg_a, conv_w[l], wco),
            w)
        zs, cbuf_s, x1s, idx_s, pw_s = _layer_group(
            xs,
            lambda z, qkv: _sample_attention(z, cache_k[l], cache_v[l], page_table, bias_rows, bias_now,
                                             lam, sub, lam_init),
            lambda z: _conv_step(z, state_conv[l], bg_a, conv_w[l], wco),
            w)
        y_prompt = _peer_down(idx_p, pw_p, v_tab, x1p, norm_final)
        y_sample = _peer_down(idx_s, pw_s, v_tab, x1s, norm_final)
        outs.append((zp, cbuf_p, zs, cbuf_s))

    zp, cbuf_p, zs, cbuf_s = outs[0]
    kc = slice(COL_K * D_MODEL, (COL_K + 1) * D_MODEL)
    vc = slice(COL_V * D_MODEL, (COL_V + 1) * D_MODEL)
    return (
        y_prompt.reshape(bsz, seqlen, D_MODEL),
        y_sample.reshape(nseq, 1, D_MODEL),
        zp[:, kc].reshape(1, bsz, seqlen, N_QK, HEAD_DIM),
        zp[:, vc].reshape(1, bsz, seqlen, N_HEADS, V_DIM),
        cbuf_p[None],
        zs[:, kc].reshape(1, nseq, 1, N_QK, HEAD_DIM),
        zs[:, vc].reshape(1, nseq, 1, N_HEADS, V_DIM),
        cbuf_s[None],
    )
```
